```python
import jax, jax.numpy as jnp
from jax import lax
import numpy as np

D_MODEL = 1024
BATCH = 8
SEQ = 2048
DEPTH = 2
DEC_BATCH = 32
DEC_SEQ = 8
PAST_LEN = 8192
PAGE_SIZE = 128

D_MIX = D_MODEL
HEAD_DIM = 64
D_ATTN = D_MIX // 2
N_HEADS_A = D_ATTN // HEAD_DIM
ROPE_THETA = 500000.0
IDX_HEADS = 8
IDX_DIM = 64
TOPK_MAX = 256
Q_BLOCK = 128
D_POOL = D_MIX // 4
POOL_WINDOWS = (2, 4, 8, 16)
N_POOL_GROUPS = len(POOL_WINDOWS)
POOL_GROUP_DIM = D_POOL // N_POOL_GROUPS
POOL_STATE = max(POOL_WINDOWS) - 1
D_SGU = D_MIX - D_ATTN - D_POOL
SGU_HEADS = 4
SGU_HEAD_DIM = D_SGU // SGU_HEADS
CHUNK = 128
EPS = 1e-6
NEG_INF = -1e30
IN_SPLIT = (D_ATTN, D_ATTN, D_ATTN, IDX_HEADS * IDX_DIM, IDX_DIM, IDX_HEADS, D_ATTN,
            D_POOL, D_POOL, D_SGU, D_SGU, D_SGU)
IN_OFFSETS = tuple(int(o) for o in np.cumsum(IN_SPLIT)[:-1])
D_IN = int(sum(IN_SPLIT))

kernel_name = 'hybrid_dsa_pool_sgu_decode_step'


def rmsnorm(x, g):
    xf = x.astype(jnp.float32)
    xf = xf * lax.rsqrt(jnp.mean(xf * xf, axis=-1, keepdims=True) + EPS)
    return (xf * g.astype(jnp.float32)).astype(x.dtype)


def layernorm(x, g):
    xf = x.astype(jnp.float32)
    xc = xf - jnp.mean(xf, axis=-1, keepdims=True)
    xf = xc * lax.rsqrt(jnp.mean(xc * xc, axis=-1, keepdims=True) + EPS)
    return (xf * g.astype(jnp.float32)).astype(x.dtype)


def rope_partial(x, pos):
    d = x.shape[-1]
    rot = d // 4
    half = rot // 2
    inv = ROPE_THETA ** (-jnp.arange(half, dtype=jnp.float32) * 2.0 / rot)
    ang = pos.astype(jnp.float32)[:, None] * inv[None, :]
    cos = jnp.cos(ang)[:, None, :]
    sin = jnp.sin(ang)[:, None, :]
    xr = x[..., :rot].astype(jnp.float32)
    x1, x2 = xr[..., :half], xr[..., half:]
    xr = jnp.concatenate([x1 * cos - x2 * sin, x2 * cos + x1 * sin], axis=-1).astype(x.dtype)
    return jnp.concatenate([xr, x[..., rot:]], axis=-1)


def project(x, pos, norm_pre_l, w_in_l):
    B, T, _ = x.shape
    h = rmsnorm(x, norm_pre_l)
    z = jnp.einsum('btd,de->bte', h, w_in_l)
    q, k, v, qi, ki, wi, ga, xb, gb, u, vc, gc = jnp.split(z, list(IN_OFFSETS), axis=-1)
    q = rope_partial(q.reshape(B, T, N_HEADS_A, HEAD_DIM), pos)
    k = rope_partial(k.reshape(B, T, N_HEADS_A, HEAD_DIM), pos)
    v = v.reshape(B, T, N_HEADS_A, HEAD_DIM)
    qi = rope_partial(qi.reshape(B, T, IDX_HEADS, IDX_DIM), pos)
    ki = rope_partial(ki[:, :, None, :], pos)[:, :, 0, :]
    wi = wi * IDX_HEADS ** -0.5
    return q, k, v, qi, ki, wi, ga, xb, gb, u, vc, gc


def dsa_select(qi, wi, ki, pos_q, topk):
    s = jnp.einsum('bqhd,bld->bqhl', qi, ki).astype(jnp.float32) * IDX_DIM ** -0.5
    score = jnp.einsum('bqhl,bqh->bql', jax.nn.relu(s), wi.astype(jnp.float32))
    key_pos = jnp.arange(ki.shape[1], dtype=jnp.int32)
    causal = key_pos[None, :] <= pos_q[:, None]
    score = jnp.where(causal[None], score, NEG_INF)
    _, idx = lax.top_k(score, topk)
    valid = idx <= pos_q[None, :, None]
    return idx, valid


def gather_rows(a, idx):
    return jax.vmap(lambda ab, ib: ab[ib])(a, idx)


def attend_selected(q, ks, vs, valid):
    logits = jnp.einsum('bqhd,bqkhd->bqhk', q, ks).astype(jnp.float32) * HEAD_DIM ** -0.5
    logits = jnp.where(valid[:, :, None, :], logits, NEG_INF)
    p = jax.nn.softmax(logits, axis=-1).astype(vs.dtype)
    return jnp.einsum('bqhk,bqkhd->bqhd', p, vs)


def prompt_sparse_attention(q, k, v, qi, ki, wi):
    B, S = q.shape[:2]
    topk = min(TOPK_MAX, S // 4)
    nb = S // Q_BLOCK

    def to_blocks(a):
        return jnp.moveaxis(a.reshape((B, nb, Q_BLOCK) + a.shape[2:]), 1, 0)

    pos_blocks = jnp.arange(S, dtype=jnp.int32).reshape(nb, Q_BLOCK)

    def one_block(args):
        qb, qib, wib, pb = args
        idx, valid = dsa_select(qib, wib, ki, pb, topk)
        return attend_selected(qb, gather_rows(k, idx), gather_rows(v, idx), valid)

    out = lax.map(one_block, (to_blocks(q), to_blocks(qi), to_blocks(wi), pos_blocks))
    return jnp.moveaxis(out, 0, 1).reshape(B, S, N_HEADS_A, HEAD_DIM)


def sample_sparse_attention(q, k_new, v_new, qi, ki_new, wi, cache_k, cache_v, cache_kidx, page_table, l):
    Bd, T = q.shape[:2]
    ps = cache_k.shape[2]
    past = page_table.shape[1] * ps
    ki_past = cache_kidx[l, page_table].reshape(Bd, past, IDX_DIM)
    ki_all = jnp.concatenate([ki_past, ki_new], axis=1)
    pos_q = past + jnp.arange(T, dtype=jnp.int32)
    topk = min(TOPK_MAX, (past + T) // 4)
    idx, valid = dsa_select(qi, wi, ki_all, pos_q, topk)
    in_past = (idx < past)[..., None, None]
    pidx = jnp.minimum(idx, past - 1)
    phys = page_table[jnp.arange(Bd)[:, None, None], pidx // ps]
    off = pidx % ps
    nidx = jnp.clip(idx - past, 0, T - 1)
    ks = jnp.where(in_past, cache_k[l, phys, off], gather_rows(k_new, nidx))
    vs = jnp.where(in_past, cache_v[l, phys, off], gather_rows(v_new, nidx))
    return attend_selected(q, ks, vs, valid)


def pool_mix(xb, state, pos, pool_w_l, pool_scale_l):
    B, T, _ = xb.shape
    P = POOL_STATE
    xp = jnp.concatenate([state.astype(xb.dtype), xb], axis=1)
    cs = jnp.cumsum(xp.astype(jnp.float32), axis=1)
    cs0 = jnp.concatenate([jnp.zeros((B, 1, D_POOL), jnp.float32), cs], axis=1)
    means = []
    for g, w in enumerate(POOL_WINDOWS):
        sl = slice(g * POOL_GROUP_DIM, (g + 1) * POOL_GROUP_DIM)
        win = cs0[:, P + 1:P + 1 + T, sl] - cs0[:, P + 1 - w:P + 1 - w + T, sl]
        cnt = jnp.minimum(w, pos + 1).astype(jnp.float32)[None, :, None]
        means.append(win / cnt)
    pooled = (jnp.concatenate(means, axis=-1) - xb.astype(jnp.float32)).astype(xb.dtype)
    pooled = pooled.reshape(B, T, N_POOL_GROUPS, POOL_GROUP_DIM)
    y = jnp.einsum('btgc,gcd->btgd', pooled, pool_w_l).reshape(B, T, D_POOL) * pool_scale_l
    return y, xp[:, -P:]


def sgu_mix(u, vc, sgu_w_l, sgu_b_l, sgu_norm_l):
    B, T, _ = u.shape
    vn = layernorm(vc, sgu_norm_l)
    cl = min(T, CHUNK)
    nc = T // cl
    mask = jnp.tril(jnp.ones((cl, cl), dtype=bool))
    w = jnp.where(mask[None], sgu_w_l[:, :cl, :cl], 0.0).astype(vn.dtype)
    vh = vn.reshape(B, nc, cl, SGU_HEADS, SGU_HEAD_DIM)
    z = jnp.einsum('hts,bcshd->bcthd', w, vh) + sgu_b_l[:, :cl].T[None, None, :, :, None]
    return u * z.reshape(B, T, D_SGU), vn


def merge(x, ya, yb, yc, ga, gb, gc, w_out_l, norm_post_l):
    B, T, _ = x.shape
    y = jnp.concatenate([ya.reshape(B, T, D_ATTN) * jax.nn.silu(ga),
                         yb * jax.nn.silu(gb),
                         yc * jax.nn.silu(gc)], axis=-1)
    o = jnp.einsum('bte,ed->btd', y, w_out_l)
    return x + rmsnorm(o, norm_post_l)


def setup_inputs(seed: int = 0) -> dict:
    key = jax.random.key(seed)
    ks = jax.random.split(key, 16)
    f32 = jnp.float32
    n_pages = PAST_LEN // PAGE_SIZE
    n_used = DEC_BATCH * n_pages
    n_pool = (n_used * 5) // 4
    page_table = jax.random.permutation(ks[0], n_pool)[:n_used].reshape(DEC_BATCH, n_pages).astype(jnp.int32)
    nrm = lambda k, s: jax.random.normal(k, s, f32)
    x_prompt = nrm(ks[1], (BATCH, SEQ, D_MODEL))
    x_sample = nrm(ks[2], (DEC_BATCH, DEC_SEQ, D_MODEL))
    cache_k = nrm(ks[3], (DEPTH, n_pool, PAGE_SIZE, N_HEADS_A, HEAD_DIM))
    cache_v = nrm(ks[4], (DEPTH, n_pool, PAGE_SIZE, N_HEADS_A, HEAD_DIM))
    cache_kidx = nrm(ks[5], (DEPTH, n_pool, PAGE_SIZE, IDX_DIM))
    state_pool = nrm(ks[6], (DEPTH, DEC_BATCH, POOL_STATE, D_POOL))
    norm_pre = 1.0 + 0.05 * nrm(ks[7], (DEPTH, D_MODEL))
    w_in = nrm(ks[8], (DEPTH, D_MODEL, D_IN)) * D_MODEL ** -0.5
    pool_w = nrm(ks[9], (DEPTH, N_POOL_GROUPS, POOL_GROUP_DIM, POOL_GROUP_DIM)) * POOL_GROUP_DIM ** -0.5
    pool_scale = 1.0 + 0.1 * nrm(ks[10], (DEPTH, D_POOL))
    sgu_w = nrm(ks[11], (DEPTH, SGU_HEADS, CHUNK, CHUNK)) * CHUNK ** -0.5
    sgu_b = 1.0 + 0.05 * nrm(ks[12], (DEPTH, SGU_HEADS, CHUNK))
    sgu_norm = 1.0 + 0.05 * nrm(ks[13], (DEPTH, D_SGU))
    w_out = nrm(ks[14], (DEPTH, D_MIX, D_MODEL)) * D_MIX ** -0.5
    norm_post = 1.0 + 0.05 * nrm(ks[15], (DEPTH, D_MODEL))
    return {'x_prompt': x_prompt, 'x_sample': x_sample, 'cache_k': cache_k, 'cache_v': cache_v,
            'cache_kidx': cache_kidx, 'state_pool': state_pool, 'page_table': page_table,
            'norm_pre': norm_pre, 'w_in': w_in, 'pool_w': pool_w, 'pool_scale': pool_scale,
            'sgu_w': sgu_w, 'sgu_b': sgu_b, 'sgu_norm': sgu_norm, 'w_out': w_out, 'norm_post': norm_post}


def reference(x_prompt, x_sample, cache_k, cache_v, cache_kidx, state_pool, page_table,
              norm_pre, w_in, pool_w, pool_scale, sgu_w, sgu_b, sgu_norm, w_out, norm_post):
    B, S, _ = x_prompt.shape
    Bd, Ts, _ = x_sample.shape
    past = page_table.shape[1] * cache_k.shape[2]
    pos_p = jnp.arange(S, dtype=jnp.int32)
    pos_s = past + jnp.arange(Ts, dtype=jnp.int32)
    xp, xs = x_prompt, x_sample
    kp_l, vp_l, kip_l, pp_l = [], [], [], []
    ks_l, vs_l, kis_l, ps_l, cvs_l = [], [], [], [], []
    for l in range(DEPTH):
        q, k, v, qi, ki, wi, ga, xb, gb, u, vc, gc = project(xp, pos_p, norm_pre[l], w_in[l])
        ya = prompt_sparse_attention(q, k, v, qi, ki, wi)
        yb, pst = pool_mix(xb, jnp.zeros((B, POOL_STATE, D_POOL), xb.dtype), pos_p, pool_w[l], pool_scale[l])
        yc, _ = sgu_mix(u, vc, sgu_w[l], sgu_b[l], sgu_norm[l])
        xp = merge(xp, ya, yb, yc, ga, gb, gc, w_out[l], norm_post[l])
        kp_l.append(k); vp_l.append(v); kip_l.append(ki); pp_l.append(pst)
        q, k, v, qi, ki, wi, ga, xb, gb, u, vc, gc = project(xs, pos_s, norm_pre[l], w_in[l])
        ya = sample_sparse_attention(q, k, v, qi, ki, wi, cache_k, cache_v, cache_kidx, page_table, l)
        yb, sst = pool_mix(xb, state_pool[l], pos_s, pool_w[l], pool_scale[l])
        yc, vn = sgu_mix(u, vc, sgu_w[l], sgu_b[l], sgu_norm[l])
        xs = merge(xs, ya, yb, yc, ga, gb, gc, w_out[l], norm_post[l])
        ks_l.append(k); vs_l.append(v); kis_l.append(ki); ps_l.append(sst); cvs_l.append(vn)
    return (xp, xs, jnp.stack(kp_l), jnp.stack(vp_l), jnp.stack(kip_l), jnp.stack(pp_l),
            jnp.stack(ks_l), jnp.stack(vs_l), jnp.stack(kis_l), jnp.stack(ps_l), jnp.stack(cvs_l))
```

```python
import functools

import numpy as np
import jax
import jax.numpy as jnp
from jax import lax
from jax.experimental import pallas as pl
from jax.experimental.pallas import tpu as pltpu

F32 = jnp.float32
BF16 = jnp.bfloat16
I32 = jnp.int32

D_MODEL = 1024
HEAD_DIM = 64
D_ATTN = 512
N_HEADS_A = 8
ROPE_THETA = 500000.0
IDX_HEADS = 8
IDX_DIM = 64
TOPK_MAX = 256
D_POOL = 256
POOL_WINDOWS = (2, 4, 8, 16)
POOL_GROUP_DIM = 64
POOL_STATE = 15
D_SGU = 256
SGU_HEADS = 4
CHUNK = 128
EPS = 1e-6
NEG_INF = -1e30
INT_MIN = -2 ** 31

N_FRONT = 4 * D_ATTN
N_KW = IDX_DIM + IDX_HEADS
N_REST = D_ATTN + 5 * D_POOL
LANES = 128
SUBLANES = 8
D_IN_PACKED = N_FRONT + LANES + N_REST
REST_XB = D_ATTN
REST_GB = REST_XB + D_POOL
REST_U = REST_GB + D_POOL
REST_VC = REST_U + D_SGU
REST_GC = REST_VC + D_SGU

VMEM_LIMIT = 48 * 1024 * 1024


def _cparams(sem):
    return pltpu.CompilerParams(dimension_semantics=sem, vmem_limit_bytes=VMEM_LIMIT)


def _tile_lanes(t, n):
    reps = n // LANES
    return t if reps == 1 else jnp.concatenate([t] * reps, axis=1)


def _rope(z, c, a, b):
    n = z.shape[1]
    return (z * _tile_lanes(c, n)
            + pltpu.roll(z, n - 8, 1) * _tile_lanes(a, n)
            + pltpu.roll(z, 8, 1) * _tile_lanes(b, n))


def _proj_kernel(x_ref, g_ref, w_ref, c_ref, a_ref, b_ref,
                 qs_ref, k_ref, kb_ref, v_ref, vb_ref, qis_ref, kiw_ref, kib2_ref, rest_ref):
    x = x_ref[0]
    ms = jnp.mean(x * x, axis=-1, keepdims=True)
    h = ((x * lax.rsqrt(ms + EPS)) * g_ref[...]).astype(BF16)
    c = c_ref[...]
    a = a_ref[...]
    b = b_ref[...]

    def mm(lo, hi):
        return jnp.dot(h, w_ref[:, lo:hi], preferred_element_type=F32)

    q = _rope(mm(0, D_ATTN), c, a, b)
    qs_ref[0] = (q * HEAD_DIM ** -0.5).astype(qs_ref.dtype)
    k = _rope(mm(D_ATTN, 2 * D_ATTN), c, a, b)
    k_ref[0] = k
    kb_ref[0] = k.astype(BF16)
    v = mm(2 * D_ATTN, 3 * D_ATTN)
    v_ref[0] = v
    vb_ref[0] = v.astype(BF16)
    qi = _rope(mm(3 * D_ATTN, 4 * D_ATTN), c, a, b)
    qis_ref[0] = (qi * IDX_DIM ** -0.5).astype(qis_ref.dtype)

    kw = mm(N_FRONT, N_FRONT + LANES)
    lane = lax.broadcasted_iota(I32, kw.shape, 1)
    is_ki = lane < IDX_DIM
    kw = (kw * jnp.where(is_ki, c, 1.0)
          + pltpu.roll(kw, LANES - 8, 1) * jnp.where(is_ki, a, 0.0)
          + pltpu.roll(kw, 8, 1) * jnp.where(is_ki, b, 0.0))
    kw = kw * jnp.where((lane >= IDX_DIM) & (lane < N_KW), IDX_HEADS ** -0.5, 1.0)
    kiw_ref[0] = kw
    kib2_ref[0] = jnp.where(is_ki, kw, pltpu.roll(kw, IDX_DIM, 1)).astype(BF16)

    base = N_FRONT + LANES
    for lo in range(0, N_REST, 256):
        rest_ref[0, :, lo:lo + 256] = mm(base + lo, base + lo + 256)


def _project(x, g, w, tabs, tm, q_dtype):
    B, T, D = x.shape
    grid = (T // tm, B)
    row = lambda i, b: (b, i, 0)
    tab = lambda i, b: (i, 0)
    const = lambda i, b: (0, 0)
    out_shape = (
        jax.ShapeDtypeStruct((B, T, D_ATTN), q_dtype),
        jax.ShapeDtypeStruct((B, T, D_ATTN), F32),
        jax.ShapeDtypeStruct((B, T, D_ATTN), BF16),
        jax.ShapeDtypeStruct((B, T, D_ATTN), F32),
        jax.ShapeDtypeStruct((B, T, D_ATTN), BF16),
        jax.ShapeDtypeStruct((B, T, D_ATTN), q_dtype),
        jax.ShapeDtypeStruct((B, T, LANES), F32),
        jax.ShapeDtypeStruct((B, T, LANES), BF16),
        jax.ShapeDtypeStruct((B, T, N_REST), F32),
    )
    widths = (D_ATTN, D_ATTN, D_ATTN, D_ATTN, D_ATTN, D_ATTN, LANES, LANES, N_REST)
    return pl.pallas_call(
        _proj_kernel,
        grid=grid,
        in_specs=[pl.BlockSpec((1, tm, D), row),
                  pl.BlockSpec((1, D), const),
                  pl.BlockSpec((D, D_IN_PACKED), const),
                  pl.BlockSpec((tm, LANES), tab),
                  pl.BlockSpec((tm, LANES), tab),
                  pl.BlockSpec((tm, LANES), tab)],
        out_specs=tuple(pl.BlockSpec((1, tm, n), row) for n in widths),
        out_shape=out_shape,
        compiler_params=_cparams(("arbitrary", "arbitrary")),
        name="project",
    )(x, g, w, *tabs)


def _div_pow2(x, n):
    assert n & (n - 1) == 0
    return lax.shift_right_logical(x, int(n).bit_length() - 1)


def _mod_pow2(x, n):
    assert n & (n - 1) == 0
    return x & (n - 1)


def _sort_key(score):
    bits = pltpu.bitcast(score, I32)
    key = jnp.where(bits < 0, bits ^ 0x7FFFFFFF, bits)
    return jnp.where(score == 0.0, 0, key)


def _half_masked(slab_bf16, hi):
    x = slab_bf16.astype(F32)
    lane = lax.broadcasted_iota(I32, x.shape, 1)
    keep = (lane >= HEAD_DIM) if hi else (lane < HEAD_DIM)
    return jnp.where(keep, x, 0.0).astype(BF16)


def _dot_nt(a, b):
    return lax.dot_general(a, b, (((1,), (1,)), ((), ())), preferred_element_type=F32)


def _attn_prompt_kernel(qs_ref, qis_ref, kiw_ref, kib2_ref, kb_ref, vb_ref, o_ref,
                        key_ref, keyT_ref, bias_ref, lg_ref, m_ref, l_ref, acc_ref, tie_ref,
                        *, topk, idx_bits):
    TQ = LANES
    i = pl.program_id(1)
    nt = i + 1
    row = lax.broadcasted_iota(I32, (TQ, TQ), 0)
    col = lax.broadcasted_iota(I32, (TQ, TQ), 1)

    def tile_off(t):
        return pl.multiple_of(t * TQ, TQ)

    def causal_mask(t):
        return col <= row + jnp.where(t < i, TQ, 0)

    qis = qis_ref[0]
    w = kiw_ref[0][:, IDX_DIM:N_KW]
    wb = [jnp.broadcast_to(w[:, h:h + 1], (TQ, TQ)) for h in range(IDX_HEADS)]
    qi_m = [_half_masked(qis[:, (h // 2) * LANES:(h // 2 + 1) * LANES], h % 2)
            for h in range(IDX_HEADS)]

    def score_tile(t, carry):
        off = tile_off(t)
        ki_t = kib2_ref[0, pl.ds(off, TQ), :]
        acc = jnp.zeros((TQ, TQ), F32)
        for h in range(IDX_HEADS):
            acc = acc + jnp.maximum(_dot_nt(qi_m[h], ki_t), 0.0) * wb[h]
        key = jnp.where(causal_mask(t), _sort_key(acc), INT_MIN)
        key_ref[:, pl.ds(off, TQ)] = key
        keyT_ref[pl.ds(off, TQ), :] = key.T
        return carry

    lax.fori_loop(0, nt, score_tile, 0)

    def count(pred):
        def body(t, acc):
            off = tile_off(t)
            m = pred(keyT_ref[pl.ds(off, TQ), :], off + row)
            return acc + jnp.sum(m.reshape(TQ // SUBLANES, SUBLANES, TQ), axis=0)
        acc = lax.fori_loop(0, nt, body, jnp.zeros((SUBLANES, TQ), F32))
        return jnp.sum(acc, axis=0, keepdims=True)

    def value_bit(j, thr):
        cand = thr ^ lax.shift_left(jnp.int32(1), 31 - j)
        cnt = count(lambda kt, l: jnp.where(kt >= cand, 1.0, 0.0))
        return jnp.where(cnt >= topk, cand, thr)

    thr = lax.fori_loop(0, 32, value_bit, jnp.full((1, TQ), INT_MIN, I32))
    n_gt = count(lambda kt, l: jnp.where(kt > thr, 1.0, 0.0))
    n_ge = count(lambda kt, l: jnp.where(kt >= thr, 1.0, 0.0))
    need = topk - n_gt

    tie_ref[...] = jnp.full(tie_ref.shape, 2 ** idx_bits, I32)

    @pl.when(jnp.max(n_ge) > topk)
    def _():
        def index_bit(j, lim):
            cand = lim | lax.shift_left(jnp.int32(1), idx_bits - 1 - j)
            cnt = count(lambda kt, l: jnp.where(kt == thr, jnp.where(l < cand, 1.0, 0.0), 0.0))
            return jnp.where(cnt <= need, cand, lim)
        lim = lax.fori_loop(0, idx_bits, index_bit, jnp.zeros((1, TQ), I32))
        tie_ref[...] = jnp.broadcast_to(lim, tie_ref.shape)

    thr_q = jnp.broadcast_to(thr, (TQ, TQ)).T
    lim_q = jnp.broadcast_to(tie_ref[0:1, :], (TQ, TQ)).T

    def bias_tile(t, carry):
        off = tile_off(t)
        k = key_ref[:, pl.ds(off, TQ)]
        tie_ok = (k == thr_q) & (off + col < lim_q)
        sel = causal_mask(t) & ((k > thr_q) | tie_ok)
        bias_ref[:, pl.ds(off, TQ)] = jnp.where(sel, 0.0, NEG_INF)
        return carry

    lax.fori_loop(0, nt, bias_tile, 0)

    qs = qs_ref[0]
    q_m = [_half_masked(qs[:, (h // 2) * LANES:(h // 2 + 1) * LANES], h % 2)
           for h in range(N_HEADS_A)]
    m_ref[...] = jnp.full(m_ref.shape, -jnp.inf, F32)

    def logits_tile(t, carry):
        off = tile_off(t)
        b_t = bias_ref[:, pl.ds(off, TQ)]
        for h in range(N_HEADS_A):
            k_t = kb_ref[0, pl.ds(off, TQ), (h // 2) * LANES:(h // 2 + 1) * LANES]
            lg = _dot_nt(q_m[h], k_t) + b_t
            lg_ref[h, :, pl.ds(off, TQ)] = lg
            m_ref[h] = jnp.maximum(m_ref[h], lg)
        return carry

    lax.fori_loop(0, nt, logits_tile, 0)

    for h in range(N_HEADS_A):
        m_ref[h] = jnp.broadcast_to(jnp.max(m_ref[h], axis=1, keepdims=True), (TQ, TQ))
    l_ref[...] = jnp.zeros(l_ref.shape, F32)
    acc_ref[...] = jnp.zeros(acc_ref.shape, F32)

    def pv_tile(t, carry):
        off = tile_off(t)
        for h in range(N_HEADS_A):
            v_t = vb_ref[0, pl.ds(off, TQ), (h // 2) * LANES:(h // 2 + 1) * LANES]
            p = jnp.exp(lg_ref[h, :, pl.ds(off, TQ)] - m_ref[h])
            l_ref[h] += p
            acc_ref[h] += jnp.dot(p.astype(BF16), v_t, preferred_element_type=F32)
        return carry

    lax.fori_loop(0, nt, pv_tile, 0)

    for j in range(N_HEADS_A // 2):
        outs = []
        for h in (2 * j, 2 * j + 1):
            l = jnp.sum(l_ref[h], axis=1, keepdims=True)
            outs.append(acc_ref[h] / l)
        o_ref[0, :, j * LANES:(j + 1) * LANES] = jnp.where(col < HEAD_DIM, outs[0], outs[1])


def _attend_prompt(qs, qis, kiw, kib2, kb, vb, topk):
    B, S, _ = qs.shape
    TQ = LANES
    nqb = S // TQ
    idx_bits = int(np.ceil(np.log2(S))) + 1
    qblk = lambda b, i: (b, i, 0)
    seq = lambda b, i: (b, 0, 0)
    return pl.pallas_call(
        functools.partial(_attn_prompt_kernel, topk=topk, idx_bits=idx_bits),
        grid=(B, nqb),
        in_specs=[pl.BlockSpec((1, TQ, D_ATTN), qblk),
                  pl.BlockSpec((1, TQ, D_ATTN), qblk),
                  pl.BlockSpec((1, TQ, LANES), qblk),
                  pl.BlockSpec((1, S, LANES), seq),
                  pl.BlockSpec((1, S, D_ATTN), seq),
                  pl.BlockSpec((1, S, D_ATTN), seq)],
        out_specs=pl.BlockSpec((1, TQ, D_ATTN), qblk),
        out_shape=jax.ShapeDtypeStruct((B, S, D_ATTN), F32),
        scratch_shapes=[pltpu.VMEM((TQ, S), I32),
                        pltpu.VMEM((S, TQ), I32),
                        pltpu.VMEM((TQ, S), F32),
                        pltpu.VMEM((N_HEADS_A, TQ, S), F32),
                        pltpu.VMEM((N_HEADS_A, TQ, TQ), F32),
                        pltpu.VMEM((N_HEADS_A, TQ, TQ), F32),
                        pltpu.VMEM((N_HEADS_A, TQ, LANES), F32),
                        pltpu.VMEM((SUBLANES, TQ), I32)],
        compiler_params=_cparams(("arbitrary", "arbitrary")),
        name="attend_prompt",
    )(qs, qis, kiw, kib2, kb, vb)


K_CHUNK = 8


def _attn_sample_kernel(pt_ref, qs_ref, qis_ref, kiw_ref, knew_ref, vnew_ref,
                        ckidx_hbm, ck_hbm, cv_hbm, o_ref,
                        kibuf, kbuf, vbuf, key_ref, bias_ref, lg_ref, sem_ki, sem_k, sem_v,
                        *, layer, topk, idx_bits, n_pages, page):
    s = pl.program_id(0)
    T = qs_ref.shape[1]
    R = N_HEADS_A * T
    past = n_pages * page
    n_chunks = n_pages // K_CHUNK

    def ki_copy(p):
        return pltpu.make_async_copy(ckidx_hbm.at[layer, pt_ref[s, p]], kibuf.at[p], sem_ki)

    def kv_copy(hbm, buf, sem, c, slot, j):
        return pltpu.make_async_copy(hbm.at[layer, pt_ref[s, c * K_CHUNK + j]],
                                     buf.at[slot, j], sem.at[slot])

    def start_chunk(hbm, buf, sem, c, slot):
        for j in range(K_CHUNK):
            kv_copy(hbm, buf, sem, c, slot, j).start()

    def wait_chunk(hbm, buf, sem, c, slot):
        for j in range(K_CHUNK):
            kv_copy(hbm, buf, sem, c, slot, j).wait()

    def start_ki(p, carry):
        ki_copy(p).start()
        return carry

    lax.fori_loop(0, n_pages, start_ki, 0)
    for c in range(2):
        start_chunk(ck_hbm, kbuf, sem_k, c, c)
        start_chunk(cv_hbm, vbuf, sem_v, c, c)

    row = lax.broadcasted_iota(I32, (T, page), 0)
    col = lax.broadcasted_iota(I32, (T, page), 1)
    new_ok = col <= row

    qis = qis_ref[0]
    parts = []
    for h in range(IDX_HEADS):
        slab = qis[:, (h // 2) * LANES:(h // 2 + 1) * LANES]
        if h % 2:
            slab = pltpu.roll(slab, HEAD_DIM, 1)
        parts.append(slab[:, :IDX_DIM])
    qi_rows = jnp.concatenate(parts, axis=0).astype(BF16)
    w = kiw_ref[0][:, IDX_DIM:N_KW]
    w_rows = jnp.concatenate([jnp.broadcast_to(w[:, h:h + 1], (T, page))
                              for h in range(IDX_HEADS)], axis=0)

    def scores(ki_page_bf16):
        r = jnp.maximum(_dot_nt(qi_rows, ki_page_bf16), 0.0) * w_rows
        return jnp.sum(r.reshape(IDX_HEADS, T, page), axis=0)

    def wait_ki(p, carry):
        ki_copy(p).wait()
        return carry

    lax.fori_loop(0, n_pages, wait_ki, 0)

    def score_page(p, carry):
        off = pl.multiple_of(p * page, page)
        key_ref[:, pl.ds(off, page)] = _sort_key(scores(kibuf[p].astype(BF16)))
        return carry

    lax.fori_loop(0, n_pages, score_page, 0)

    zpad = jnp.zeros((page - T, LANES), F32)
    ki_new = jnp.concatenate([kiw_ref[0], zpad], axis=0)[:, :IDX_DIM].astype(BF16)
    key_ref[:, past:past + page] = jnp.where(new_ok, _sort_key(scores(ki_new)), INT_MIN)

    keys = key_ref[...]
    lidx = lax.broadcasted_iota(I32, keys.shape, 1)

    def count(m):
        return jnp.sum(m, axis=1, keepdims=True)

    def value_bit(j, thr):
        cand = thr ^ lax.shift_left(jnp.int32(1), 31 - j)
        cnt = count(jnp.where(keys >= cand, 1.0, 0.0))
        return jnp.where(cnt >= topk, cand, thr)

    thr = lax.fori_loop(0, 32, value_bit, jnp.full((T, 1), INT_MIN, I32))
    n_gt = count(jnp.where(keys > thr, 1.0, 0.0))
    n_ge = count(jnp.where(keys >= thr, 1.0, 0.0))
    need = topk - n_gt

    def index_bit(j, lim):
        cand = lim | lax.shift_left(jnp.int32(1), idx_bits - 1 - j)
        cnt = count(jnp.where(keys == thr, jnp.where(lidx < cand, 1.0, 0.0), 0.0))
        return jnp.where(cnt <= need, cand, lim)

    lim = lax.cond(jnp.max(n_ge) > topk,
                   lambda: lax.fori_loop(0, idx_bits, index_bit, jnp.zeros((T, 1), I32)),
                   lambda: jnp.full((T, 1), 2 ** idx_bits, I32))
    visible = (lidx < past) | (lidx - past <= lax.broadcasted_iota(I32, keys.shape, 0))
    tie_ok = (keys == thr) & (lidx < lim)
    sel = visible & ((keys > thr) | tie_ok)
    bias_ref[...] = jnp.where(sel, 0.0, NEG_INF)

    qs = qs_ref[0]
    q_rows = jnp.concatenate([qs] * N_HEADS_A, axis=0)
    rr = lax.broadcasted_iota(I32, q_rows.shape, 0)
    cc = lax.broadcasted_iota(I32, q_rows.shape, 1)
    head_blk = _div_pow2(rr, T) == _div_pow2(cc, HEAD_DIM)
    q_bd = jnp.where(head_blk, q_rows, 0.0).astype(BF16)

    def logits_page(k_page_bf16, off, m):
        b = bias_ref[:, pl.ds(off, page)]
        lg = _dot_nt(q_bd, k_page_bf16) + jnp.concatenate([b] * N_HEADS_A, axis=0)
        lg_ref[:, pl.ds(off, page)] = lg
        return jnp.maximum(m, lg)

    m = jnp.full((R, page), -jnp.inf, F32)
    for c in range(n_chunks):
        slot = c % 2
        wait_chunk(ck_hbm, kbuf, sem_k, c, slot)

        def k_page(j, m, c=c, slot=slot):
            off = pl.multiple_of((c * K_CHUNK + j) * page, page)
            return logits_page(kbuf[slot, j].astype(BF16), off, m)

        m = lax.fori_loop(0, K_CHUNK, k_page, m)
        if c + 2 < n_chunks:
            start_chunk(ck_hbm, kbuf, sem_k, c + 2, slot)

    zrows = jnp.zeros((page - T, D_ATTN), F32)
    k_new = jnp.concatenate([knew_ref[0], zrows], axis=0).astype(BF16)
    m = logits_page(k_new, past, m)
    m_b = jnp.broadcast_to(jnp.max(m, axis=1, keepdims=True), (R, page))

    def pv_page(v_page_bf16, off, carry):
        l, acc = carry
        p = jnp.exp(lg_ref[:, pl.ds(off, page)] - m_b)
        return l + p, acc + jnp.dot(p.astype(BF16), v_page_bf16, preferred_element_type=F32)

    carry = (jnp.zeros((R, page), F32), jnp.zeros((R, D_ATTN), F32))
    for c in range(n_chunks):
        slot = c % 2
        wait_chunk(cv_hbm, vbuf, sem_v, c, slot)

        def v_page(j, carry, c=c, slot=slot):
            off = pl.multiple_of((c * K_CHUNK + j) * page, page)
            return pv_page(vbuf[slot, j].astype(BF16), off, carry)

        carry = lax.fori_loop(0, K_CHUNK, v_page, carry)
        if c + 2 < n_chunks:
            start_chunk(cv_hbm, vbuf, sem_v, c + 2, slot)

    v_new = jnp.concatenate([vnew_ref[0], zrows], axis=0).astype(BF16)
    l, acc = pv_page(v_new, past, carry)
    y = jnp.where(head_blk, acc / jnp.sum(l, axis=1, keepdims=True), 0.0)
    o_ref[0] = jnp.sum(y.reshape(N_HEADS_A, T, D_ATTN), axis=0)


def _attend_sample(page_table, qs, qis, kiw, k_new, v_new, cache_kidx, cache_k, cache_v, layer):
    Bd, T, _ = qs.shape
    n_pages = page_table.shape[1]
    page = cache_k.shape[2]
    past = n_pages * page
    n_keys = past + page
    topk = min(TOPK_MAX, (past + T) // 4)
    idx_bits = int(np.ceil(np.log2(n_keys))) + 1
    R = N_HEADS_A * T
    blk = lambda s, pt: (s, 0, 0)
    grid_spec = pltpu.PrefetchScalarGridSpec(
        num_scalar_prefetch=1,
        grid=(Bd,),
        in_specs=[pl.BlockSpec((1, T, D_ATTN), blk),
                  pl.BlockSpec((1, T, D_ATTN), blk),
                  pl.BlockSpec((1, T, LANES), blk),
                  pl.BlockSpec((1, T, D_ATTN), blk),
                  pl.BlockSpec((1, T, D_ATTN), blk),
                  pl.BlockSpec(memory_space=pl.ANY),
                  pl.BlockSpec(memory_space=pl.ANY),
                  pl.BlockSpec(memory_space=pl.ANY)],
        out_specs=pl.BlockSpec((1, T, D_ATTN), blk),
        scratch_shapes=[pltpu.VMEM((n_pages, page, IDX_DIM), F32),
                        pltpu.VMEM((2, K_CHUNK, page, D_ATTN), F32),
                        pltpu.VMEM((2, K_CHUNK, page, D_ATTN), F32),
                        pltpu.VMEM((T, n_keys), I32),
                        pltpu.VMEM((T, n_keys), F32),
                        pltpu.VMEM((R, n_keys), F32),
                        pltpu.SemaphoreType.DMA(()),
                        pltpu.SemaphoreType.DMA((2,)),
                        pltpu.SemaphoreType.DMA((2,))],
    )
    return pl.pallas_call(
        functools.partial(_attn_sample_kernel, layer=layer, topk=topk, idx_bits=idx_bits,
                          n_pages=n_pages, page=page),
        grid_spec=grid_spec,
        out_shape=jax.ShapeDtypeStruct((Bd, T, D_ATTN), F32),
        compiler_params=_cparams(("arbitrary",)),
        name="attend_sample",
    )(page_table, qs, qis, kiw, k_new, v_new, cache_kidx, cache_k, cache_v)


def _window_sums(xp):
    s2 = xp + pltpu.roll(xp, 1, 0)
    s4 = s2 + pltpu.roll(s2, 2, 0)
    s8 = s4 + pltpu.roll(s4, 4, 0)
    s16 = s8 + pltpu.roll(s8, 8, 0)
    return s2, s4, s8, s16


def _pool_select(sums, lane):
    g = _div_pow2(lane, POOL_GROUP_DIM)
    win = jnp.where(g == 0, sums[0], jnp.where(g == 1, sums[1], jnp.where(g == 2, sums[2], sums[3])))
    wlen = jnp.where(g == 0, POOL_WINDOWS[0],
                     jnp.where(g == 1, POOL_WINDOWS[1],
                               jnp.where(g == 2, POOL_WINDOWS[2], POOL_WINDOWS[3])))
    return win, wlen


def _layernorm(x, g):
    xc = x - jnp.mean(x, axis=-1, keepdims=True)
    return (xc * lax.rsqrt(jnp.mean(xc * xc, axis=-1, keepdims=True) + EPS)) * g


def _silu(g):
    return g * (1.0 / (1.0 + jnp.exp(-g)))


def _gate_project(x, ya, yb, yc, rest, wo_ref, np_ref):
    ga = rest[:, 0:D_ATTN]
    gb = rest[:, REST_GB:REST_GB + D_POOL]
    gc = rest[:, REST_GC:REST_GC + D_SGU]
    ycat = jnp.concatenate([ya * _silu(ga), yb * _silu(gb), yc * _silu(gc)], axis=1).astype(BF16)
    o = jnp.dot(ycat, wo_ref[...], preferred_element_type=F32)
    on = (o * lax.rsqrt(jnp.mean(o * o, axis=-1, keepdims=True) + EPS)) * np_ref[...]
    return x + on


def _merge_prompt_kernel(x_ref, ya_ref, rest_ref, halo_ref, pw_ref, ps_ref, sw_ref, sbt_ref,
                         sn_ref, wo_ref, np_ref, y_ref, *, tm, halo_rows):
    i = pl.program_id(1)
    rest = rest_ref[0]
    xb = rest[:, REST_XB:REST_XB + D_POOL]
    u = rest[:, REST_U:REST_U + D_SGU]
    vc = rest[:, REST_VC:REST_VC + D_SGU]

    halo = jnp.where(i > 0, halo_ref[0], 0.0)
    sums = _window_sums(jnp.concatenate([halo, xb], axis=0))
    sums = [s[halo_rows:] for s in sums]
    lane = lax.broadcasted_iota(I32, (tm, D_POOL), 1)
    pos = i * tm + lax.broadcasted_iota(I32, (tm, D_POOL), 0)
    win, wlen = _pool_select(sums, lane)
    cnt = jnp.minimum(wlen, pos + 1).astype(F32)
    pooled = (win / cnt - xb).astype(BF16)
    yb = jnp.dot(pooled, pw_ref[...], preferred_element_type=F32) * ps_ref[...]

    vn = _layernorm(vc, sn_ref[...]).astype(BF16)
    rr = lax.broadcasted_iota(I32, (CHUNK, CHUNK), 0)
    cc = lax.broadcasted_iota(I32, (CHUNK, CHUNK), 1)
    w_tril = [jnp.where(cc <= rr, sw_ref[h], 0.0).astype(BF16) for h in range(SGU_HEADS)]
    b_col = [jnp.broadcast_to(sbt_ref[:, h:h + 1], (CHUNK, LANES)) for h in range(SGU_HEADS)]
    lane_c = lax.broadcasted_iota(I32, (CHUNK, LANES), 1)
    z_chunks = []
    for c in range(tm // CHUNK):
        pairs = []
        for j in range(SGU_HEADS // 2):
            vpair = vn[c * CHUNK:(c + 1) * CHUNK, j * LANES:(j + 1) * LANES]
            z0 = jnp.dot(w_tril[2 * j], vpair, preferred_element_type=F32) + b_col[2 * j]
            z1 = jnp.dot(w_tril[2 * j + 1], vpair, preferred_element_type=F32) + b_col[2 * j + 1]
            pairs.append(jnp.where(lane_c < 64, z0, z1))
        z_chunks.append(jnp.concatenate(pairs, axis=1))
    yc = u * jnp.concatenate(z_chunks, axis=0)

    y_ref[0] = _gate_project(x_ref[0], ya_ref[0], yb, yc, rest, wo_ref, np_ref)


def _merge_prompt(x, ya, rest, pw_bd, ps, sw, sbt, sn, wo, npost, tm):
    B, S, D = x.shape
    halo_rows = 16
    row = lambda b, i: (b, i, 0)
    halo = lambda b, i: (b, jnp.maximum(i * (tm // halo_rows) - 1, 0), REST_XB // D_POOL)
    c2 = lambda b, i: (0, 0)
    c3 = lambda b, i: (0, 0, 0)
    return pl.pallas_call(
        functools.partial(_merge_prompt_kernel, tm=tm, halo_rows=halo_rows),
        grid=(B, S // tm),
        in_specs=[pl.BlockSpec((1, tm, D), row),
                  pl.BlockSpec((1, tm, D_ATTN), row),
                  pl.BlockSpec((1, tm, N_REST), row),
                  pl.BlockSpec((1, halo_rows, D_POOL), halo),
                  pl.BlockSpec((D_POOL, D_POOL), c2),
                  pl.BlockSpec((1, D_POOL), c2),
                  pl.BlockSpec((SGU_HEADS, CHUNK, CHUNK), c3),
                  pl.BlockSpec((CHUNK, SGU_HEADS), c2),
                  pl.BlockSpec((1, D_SGU), c2),
                  pl.BlockSpec((D, D), c2),
                  pl.BlockSpec((1, D), c2)],
        out_specs=pl.BlockSpec((1, tm, D), row),
        out_shape=jax.ShapeDtypeStruct((B, S, D), F32),
        compiler_params=_cparams(("arbitrary", "arbitrary")),
        name="merge_prompt",
    )(x, ya, rest, rest, pw_bd, ps, sw, sbt, sn, wo, npost)


def _merge_sample_kernel(x_ref, ya_ref, rest_ref, st_ref, pw_ref, ps_ref, swt_ref, sbt_ref,
                         sn_ref, wo_ref, np_ref, y_ref, vn_ref, *, n_seq, t_new, pos0):
    rows = n_seq * t_new
    pre = st_ref.shape[1]
    rest = rest_ref[...]
    xb = rest[:, REST_XB:REST_XB + D_POOL]
    u = rest[:, REST_U:REST_U + D_SGU]
    vc = rest[:, REST_VC:REST_VC + D_SGU]

    xp = jnp.concatenate([st_ref[...], xb.reshape(n_seq, t_new, D_POOL)], axis=1)
    sums = _window_sums(xp.reshape(n_seq * (pre + t_new), D_POOL))
    sums = [s.reshape(n_seq, pre + t_new, D_POOL)[:, pre:].reshape(rows, D_POOL) for s in sums]
    lane = lax.broadcasted_iota(I32, (rows, D_POOL), 1)
    r = lax.broadcasted_iota(I32, (rows, D_POOL), 0)
    win, wlen = _pool_select(sums, lane)
    cnt = jnp.minimum(wlen, pos0 + _mod_pow2(r, t_new) + 1).astype(F32)
    pooled = (win / cnt - xb).astype(BF16)
    yb = jnp.dot(pooled, pw_ref[...], preferred_element_type=F32) * ps_ref[...]

    vn = _layernorm(vc, sn_ref[...])
    vn_ref[...] = vn
    vnb = vn.astype(BF16)
    rr = lax.broadcasted_iota(I32, (rows, rows), 0)
    cc = lax.broadcasted_iota(I32, (rows, rows), 1)
    keep = ((_div_pow2(rr, t_new) == _div_pow2(cc, t_new))
            & (_mod_pow2(cc, t_new) <= _mod_pow2(rr, t_new)))
    lane_c = lax.broadcasted_iota(I32, (rows, LANES), 1)
    pairs = []
    for j in range(SGU_HEADS // 2):
        vpair = vnb[:, j * LANES:(j + 1) * LANES]
        zs = []
        for h in (2 * j, 2 * j + 1):
            wt = jnp.where(keep, swt_ref[h], 0.0).astype(BF16)
            zs.append(jnp.dot(wt, vpair, preferred_element_type=F32)
                      + jnp.broadcast_to(sbt_ref[:, h:h + 1], (rows, LANES)))
        pairs.append(jnp.where(lane_c < 64, zs[0], zs[1]))
    yc = u * jnp.concatenate(pairs, axis=1)

    y_ref[...] = _gate_project(x_ref[...], ya_ref[...], yb, yc, rest, wo_ref, np_ref)


def _merge_sample(x, ya, rest, state16, pw_bd, ps, sw_tiled, sbt_tiled, sn, wo, npost,
                  n_seq, t_new, pos0):
    rows, D = x.shape
    full2 = lambda shape: pl.BlockSpec(shape, lambda i: (0, 0))
    full3 = lambda shape: pl.BlockSpec(shape, lambda i: (0, 0, 0))
    return pl.pallas_call(
        functools.partial(_merge_sample_kernel, n_seq=n_seq, t_new=t_new, pos0=pos0),
        grid=(1,),
        in_specs=[full2((rows, D)), full2((rows, D_ATTN)), full2((rows, N_REST)),
                  full3(state16.shape), full2((D_POOL, D_POOL)), full2((1, D_POOL)),
                  full3(sw_tiled.shape), full2(sbt_tiled.shape), full2((1, D_SGU)),
                  full2((D, D)), full2((1, D))],
        out_specs=(full2((rows, D)), full2((rows, D_SGU))),
        out_shape=(jax.ShapeDtypeStruct((rows, D), F32),
                   jax.ShapeDtypeStruct((rows, D_SGU), F32)),
        compiler_params=_cparams(("arbitrary",)),
        name="merge_sample",
    )(x, ya, rest, state16, pw_bd, ps, sw_tiled, sbt_tiled, sn, wo, npost)


def _rope_tables(pos):
    rot = HEAD_DIM // 4
    half = rot // 2
    n = pos.shape[0]
    inv = ROPE_THETA ** (-jnp.arange(half, dtype=F32) * 2.0 / rot)
    ang = pos.astype(F32)[:, None] * inv[None, :]
    cos, sin = jnp.cos(ang), jnp.sin(ang)
    one = jnp.ones((n, HEAD_DIM - rot), F32)
    zero = jnp.zeros((n, HEAD_DIM - rot), F32)
    z8 = jnp.zeros((n, half), F32)
    c = jnp.concatenate([cos, cos, one], axis=1)
    a = jnp.concatenate([-sin, z8, zero], axis=1)
    b = jnp.concatenate([z8, sin, zero], axis=1)
    return tuple(jnp.concatenate([t, t], axis=1) for t in (c, a, b))


def _block_diag(w):
    G, n, _ = w.shape
    eye = jnp.eye(G, dtype=w.dtype)
    return (eye[:, None, :, None] * w[:, :, None, :]).reshape(G * n, G * n)


def kernel(x_prompt, x_sample, cache_k, cache_v, cache_kidx, state_pool, page_table, norm_pre,
           w_in, pool_w, pool_scale, sgu_w, sgu_b, sgu_norm, w_out, norm_post):
    B, S, D = x_prompt.shape
    Bd, Ts, _ = x_sample.shape
    depth = w_in.shape[0]
    n_pool, page = cache_k.shape[1], cache_k.shape[2]
    past = page_table.shape[1] * page
    rows_s = Bd * Ts

    tabs_p = _rope_tables(jnp.arange(S, dtype=I32))
    tabs_s = tuple(jnp.tile(t, (Bd, 1)) for t in _rope_tables(past + jnp.arange(Ts, dtype=I32)))

    n_kw_end = N_FRONT + N_KW
    w_packed = jnp.concatenate(
        [w_in[:, :, :n_kw_end], jnp.zeros((depth, D, LANES - N_KW), w_in.dtype), w_in[:, :, n_kw_end:]],
        axis=2).astype(BF16)
    w_out_b = w_out.astype(BF16)
    ck = cache_k.reshape(depth, n_pool, page, D_ATTN)
    cv = cache_v.reshape(depth, n_pool, page, D_ATTN)

    xp = x_prompt
    xs = x_sample.reshape(1, rows_s, D)
    outs = {n: [] for n in ("kp", "vp", "kip", "pp", "ks", "vs", "kis", "ps", "cvs")}
    topk_p = min(TOPK_MAX, S // 4)
    for l in range(depth):
        g_pre = norm_pre[l].reshape(1, D)
        g_post = norm_post[l].reshape(1, D)
        pw_bd = _block_diag(pool_w[l]).astype(BF16)
        ps = pool_scale[l].reshape(1, D_POOL)
        sn = sgu_norm[l].reshape(1, D_SGU)

        qs, k, kb, v, vb, qis, kiw, kib2, rest = _project(xp, g_pre, w_packed[l], tabs_p, 256, BF16)
        ya = _attend_prompt(qs, qis, kiw, kib2, kb, vb, topk_p)
        xp = _merge_prompt(xp, ya, rest, pw_bd, ps, sgu_w[l], sgu_b[l].T, sn, w_out_b[l], g_post, 256)
        outs["kp"].append(k.reshape(B, S, N_HEADS_A, HEAD_DIM))
        outs["vp"].append(v.reshape(B, S, N_HEADS_A, HEAD_DIM))
        outs["kip"].append(kiw[:, :, :IDX_DIM])
        outs["pp"].append(rest[:, S - POOL_STATE:, REST_XB:REST_XB + D_POOL])

        qs, k, kb, v, vb, qis, kiw, kib2, rest = _project(xs, g_pre, w_packed[l], tabs_s, rows_s, F32)
        seq = lambda t: t.reshape(Bd, Ts, t.shape[-1])
        ya = _attend_sample(page_table, seq(qs), seq(qis), seq(kiw), seq(k), seq(v),
                            cache_kidx, ck, cv, l)
        state16 = jnp.pad(state_pool[l], ((0, 0), (1, 0), (0, 0)))
        sw_tiled = jnp.tile(sgu_w[l][:, :Ts, :Ts], (1, Bd, Bd))
        sbt_tiled = jnp.tile(sgu_b[l][:, :Ts].T, (Bd, 1))
        xs2, vn = _merge_sample(xs[0], ya.reshape(rows_s, D_ATTN), rest[0], state16, pw_bd, ps,
                                sw_tiled, sbt_tiled, sn, w_out_b[l], g_post, Bd, Ts, past)
        xs = xs2.reshape(1, rows_s, D)
        xb_s = rest[0][:, REST_XB:REST_XB + D_POOL].reshape(Bd, Ts, D_POOL)
        outs["ks"].append(k.reshape(Bd, Ts, N_HEADS_A, HEAD_DIM))
        outs["vs"].append(v.reshape(Bd, Ts, N_HEADS_A, HEAD_DIM))
        outs["kis"].append(kiw[0][:, :IDX_DIM].reshape(Bd, Ts, IDX_DIM))
        outs["ps"].append(jnp.concatenate([state_pool[l], xb_s], axis=1)[:, -POOL_STATE:])
        outs["cvs"].append(vn.reshape(Bd, Ts, D_SGU))

    st = lambda n: jnp.stack(outs[n])
    return (xp, xs.reshape(Bd, Ts, D), st("kp"), st("vp"), st("kip"), st("pp"),
            st("ks"), st("vs"), st("kis"), st("ps"), st("cvs"))
```

```python
import functools

import numpy as np
import jax
import jax.numpy as jnp
from jax import lax
from jax.experimental import pallas as pl
from jax.experimental.pallas import tpu as pltpu

F32 = jnp.float32
BF16 = jnp.bfloat16
I32 = jnp.int32

D_MODEL = 1024
HEAD_DIM = 64
D_ATTN = 512
N_HEADS_A = 8
ROPE_THETA = 500000.0
IDX_HEADS = 8
IDX_DIM = 64
TOPK_MAX = 256
D_POOL = 256
POOL_WINDOWS = (2, 4, 8, 16)
POOL_GROUP_DIM = 64
POOL_STATE = 15
D_SGU = 256
SGU_HEADS = 4
CHUNK = 128
EPS = 1e-6
NEG_INF = -1e30
INT_MIN = -2 ** 31

N_FRONT = 4 * D_ATTN
N_KW = IDX_DIM + IDX_HEADS
N_REST = D_ATTN + 5 * D_POOL
LANES = 128
SUBLANES = 8
D_IN_PACKED = N_FRONT + LANES + N_REST
REST_XB = D_ATTN
REST_GB = REST_XB + D_POOL
REST_U = REST_GB + D_POOL
REST_VC = REST_U + D_SGU
REST_GC = REST_VC + D_SGU

VMEM_LIMIT = 48 * 1024 * 1024


def _cparams(sem):
    return pltpu.CompilerParams(dimension_semantics=sem, vmem_limit_bytes=VMEM_LIMIT)


def _tile_lanes(t, n):
    reps = n // LANES
    return t if reps == 1 else jnp.concatenate([t] * reps, axis=1)


def _rope(z, c, a, b):
    n = z.shape[1]
    return (z * _tile_lanes(c, n)
            + pltpu.roll(z, n - 8, 1) * _tile_lanes(a, n)
            + pltpu.roll(z, 8, 1) * _tile_lanes(b, n))


def _proj_kernel(x_ref, g_ref, w_ref, c_ref, a_ref, b_ref,
                 qs_ref, k_ref, kb_ref, v_ref, vb_ref, qis_ref, kiw_ref, kib2_ref, rest_ref):
    x = x_ref[0]
    ms = jnp.mean(x * x, axis=-1, keepdims=True)
    h = ((x * lax.rsqrt(ms + EPS)) * g_ref[...]).astype(BF16)
    c = c_ref[...]
    a = a_ref[...]
    b = b_ref[...]

    def mm(lo, hi):
        return jnp.dot(h, w_ref[:, lo:hi], preferred_element_type=F32)

    q = _rope(mm(0, D_ATTN), c, a, b)
    qs_ref[0] = (q * HEAD_DIM ** -0.5).astype(qs_ref.dtype)
    k = _rope(mm(D_ATTN, 2 * D_ATTN), c, a, b)
    k_ref[0] = k
    kb_ref[0] = k.astype(BF16)
    v = mm(2 * D_ATTN, 3 * D_ATTN)
    v_ref[0] = v
    vb_ref[0] = v.astype(BF16)
    qi = _rope(mm(3 * D_ATTN, 4 * D_ATTN), c, a, b)
    qis_ref[0] = (qi * IDX_DIM ** -0.5).astype(qis_ref.dtype)

    kw = mm(N_FRONT, N_FRONT + LANES)
    lane = lax.broadcasted_iota(I32, kw.shape, 1)
    is_ki = lane < IDX_DIM
    kw = (kw * jnp.where(is_ki, c, 1.0)
          + pltpu.roll(kw, LANES - 8, 1) * jnp.where(is_ki, a, 0.0)
          + pltpu.roll(kw, 8, 1) * jnp.where(is_ki, b, 0.0))
    kw = kw * jnp.where((lane >= IDX_DIM) & (lane < N_KW), IDX_HEADS ** -0.5, 1.0)
    kiw_ref[0] = kw
    kib2_ref[0] = jnp.where(is_ki, kw, pltpu.roll(kw, IDX_DIM, 1)).astype(BF16)

    base = N_FRONT + LANES
    for lo in range(0, N_REST, 256):
        rest_ref[0, :, lo:lo + 256] = mm(base + lo, base + lo + 256)


def _project(x, g, w, tabs, tm, q_dtype):
    B, T, D = x.shape
    grid = (T // tm, B)
    row = lambda i, b: (b, i, 0)
    tab = lambda i, b: (i, 0)
    const = lambda i, b: (0, 0)
    out_shape = (
        jax.ShapeDtypeStruct((B, T, D_ATTN), q_dtype),
        jax.ShapeDtypeStruct((B, T, D_ATTN), F32),
        jax.ShapeDtypeStruct((B, T, D_ATTN), BF16),
        jax.ShapeDtypeStruct((B, T, D_ATTN), F32),
        jax.ShapeDtypeStruct((B, T, D_ATTN), BF16),
        jax.ShapeDtypeStruct((B, T, D_ATTN), q_dtype),
        jax.ShapeDtypeStruct((B, T, LANES), F32),
        jax.ShapeDtypeStruct((B, T, LANES), BF16),
        jax.ShapeDtypeStruct((B, T, N_REST), F32),
    )
    widths = (D_ATTN, D_ATTN, D_ATTN, D_ATTN, D_ATTN, D_ATTN, LANES, LANES, N_REST)
    return pl.pallas_call(
        _proj_kernel,
        grid=grid,
        in_specs=[pl.BlockSpec((1, tm, D), row),
                  pl.BlockSpec((1, D), const),
                  pl.BlockSpec((D, D_IN_PACKED), const),
                  pl.BlockSpec((tm, LANES), tab),
                  pl.BlockSpec((tm, LANES), tab),
                  pl.BlockSpec((tm, LANES), tab)],
        out_specs=tuple(pl.BlockSpec((1, tm, n), row) for n in widths),
        out_shape=out_shape,
        compiler_params=_cparams(("arbitrary", "arbitrary")),
        name="project",
    )(x, g, w, *tabs)


def _div_pow2(x, n):
    assert n & (n - 1) == 0
    return lax.shift_right_logical(x, int(n).bit_length() - 1)


def _mod_pow2(x, n):
    assert n & (n - 1) == 0
    return x & (n - 1)


def _sort_key(score):
    bits = pltpu.bitcast(score, I32)
    key = jnp.where(bits < 0, bits ^ 0x7FFFFFFF, bits)
    return jnp.where(score == 0.0, 0, key)


def _half_masked(slab_bf16, hi):
    x = slab_bf16.astype(F32)
    lane = lax.broadcasted_iota(I32, x.shape, 1)
    keep = (lane >= HEAD_DIM) if hi else (lane < HEAD_DIM)
    return jnp.where(keep, x, 0.0).astype(BF16)


def _dot_nt(a, b):
    return lax.dot_general(a, b, (((1,), (1,)), ((), ())), preferred_element_type=F32)


KEY_GROUP = 512


def _attn_prompt_kernel(qs_ref, qis_ref, kiw_ref, kib2_ref, kb_ref, vb_ref, o_ref,
                        key_ref, keyT_ref, bias_ref, lg_ref, m_ref, l_ref, acc_ref, tie_ref,
                        *, topk, idx_bits):
    TQ = LANES
    NC = KEY_GROUP // TQ
    i = pl.program_id(1)
    ng = lax.shift_right_logical(i, NC.bit_length() - 1) + 1
    row = lax.broadcasted_iota(I32, (TQ, KEY_GROUP), 0)
    col = lax.broadcasted_iota(I32, (TQ, KEY_GROUP), 1)
    krow = lax.broadcasted_iota(I32, (KEY_GROUP, TQ), 0)

    def group_off(g):
        return pl.multiple_of(g * KEY_GROUP, KEY_GROUP)

    def causal_mask(g):
        return col <= row + (i * TQ - g * KEY_GROUP)

    def lane_tiles(x):
        return [x[:, c * TQ:(c + 1) * TQ] for c in range(NC)]

    qis = qis_ref[0]
    w = kiw_ref[0][:, IDX_DIM:N_KW]
    wb = [jnp.broadcast_to(w[:, h:h + 1], (TQ, TQ)) for h in range(IDX_HEADS)]
    qi_m = [_half_masked(qis[:, (h // 2) * LANES:(h // 2 + 1) * LANES], h % 2)
            for h in range(IDX_HEADS)]

    def score_group(g, carry):
        off = group_off(g)
        ki_g = kib2_ref[0, pl.ds(off, KEY_GROUP), :]
        acc = [jnp.zeros((TQ, TQ), F32) for _ in range(NC)]
        for h in range(IDX_HEADS):
            s = lane_tiles(_dot_nt(qi_m[h], ki_g))
            acc = [a + jnp.maximum(sc, 0.0) * wb[h] for a, sc in zip(acc, s)]
        key = jnp.where(causal_mask(g), _sort_key(jnp.concatenate(acc, axis=1)), INT_MIN)
        key_ref[:, pl.ds(off, KEY_GROUP)] = key
        for c, kc in enumerate(lane_tiles(key)):
            keyT_ref[pl.ds(off + c * TQ, TQ), :] = kc.T
        return carry

    lax.fori_loop(0, ng, score_group, 0)

    def count(pred):
        def body(g, acc):
            off = group_off(g)
            m = pred(keyT_ref[pl.ds(off, KEY_GROUP), :], off + krow)
            parts = [m[r * SUBLANES:(r + 1) * SUBLANES] for r in range(KEY_GROUP // SUBLANES)]
            while len(parts) > 1:
                parts = [a + b for a, b in zip(parts[::2], parts[1::2])]
            return acc + parts[0]
        acc = lax.fori_loop(0, ng, body, jnp.zeros((SUBLANES, TQ), F32))
        return jnp.sum(acc, axis=0, keepdims=True)

    def value_bit(j, thr):
        cand = thr ^ lax.shift_left(jnp.int32(1), 31 - j)
        cnt = count(lambda kt, l: jnp.where(kt >= cand, 1.0, 0.0))
        return jnp.where(cnt >= topk, cand, thr)

    thr = lax.fori_loop(0, 32, value_bit, jnp.full((1, TQ), INT_MIN, I32))
    n_gt = count(lambda kt, l: jnp.where(kt > thr, 1.0, 0.0))
    n_ge = count(lambda kt, l: jnp.where(kt >= thr, 1.0, 0.0))
    need = topk - n_gt

    tie_ref[...] = jnp.full(tie_ref.shape, 2 ** idx_bits, I32)

    @pl.when(jnp.max(n_ge) > topk)
    def _():
        def index_bit(j, lim):
            cand = lim | lax.shift_left(jnp.int32(1), idx_bits - 1 - j)
            cnt = count(lambda kt, l: jnp.where(kt == thr, jnp.where(l < cand, 1.0, 0.0), 0.0))
            return jnp.where(cnt <= need, cand, lim)
        lim = lax.fori_loop(0, idx_bits, index_bit, jnp.zeros((1, TQ), I32))
        tie_ref[...] = jnp.broadcast_to(lim, tie_ref.shape)

    thr_q = jnp.broadcast_to(thr, (TQ, TQ)).T[:, :1]
    lim_q = jnp.broadcast_to(tie_ref[0:1, :], (TQ, TQ)).T[:, :1]

    def bias_group(g, carry):
        off = group_off(g)
        k = key_ref[:, pl.ds(off, KEY_GROUP)]
        tie_ok = (k == thr_q) & (off + col < lim_q)
        sel = causal_mask(g) & ((k > thr_q) | tie_ok)
        bias_ref[:, pl.ds(off, KEY_GROUP)] = jnp.where(sel, 0.0, NEG_INF)
        return carry

    lax.fori_loop(0, ng, bias_group, 0)

    qs = qs_ref[0]
    q_m = [_half_masked(qs[:, (h // 2) * LANES:(h // 2 + 1) * LANES], h % 2)
           for h in range(N_HEADS_A)]
    m_ref[...] = jnp.full(m_ref.shape, -jnp.inf, F32)

    def logits_group(g, carry):
        off = group_off(g)
        b_g = bias_ref[:, pl.ds(off, KEY_GROUP)]
        for h in range(N_HEADS_A):
            k_g = kb_ref[0, pl.ds(off, KEY_GROUP), (h // 2) * LANES:(h // 2 + 1) * LANES]
            lg = _dot_nt(q_m[h], k_g) + b_g
            lg_ref[h, :, pl.ds(off, KEY_GROUP)] = lg
            m = m_ref[h]
            for lc in lane_tiles(lg):
                m = jnp.maximum(m, lc)
            m_ref[h] = m
        return carry

    lax.fori_loop(0, ng, logits_group, 0)

    for h in range(N_HEADS_A):
        m_ref[h] = jnp.broadcast_to(jnp.max(m_ref[h], axis=1, keepdims=True), (TQ, TQ))
    l_ref[...] = jnp.zeros(l_ref.shape, F32)
    acc_ref[...] = jnp.zeros(acc_ref.shape, F32)

    def pv_group(g, carry):
        off = group_off(g)
        for h in range(N_HEADS_A):
            v_g = vb_ref[0, pl.ds(off, KEY_GROUP), (h // 2) * LANES:(h // 2 + 1) * LANES]
            m = m_ref[h]
            ps = [jnp.exp(lc - m) for lc in lane_tiles(lg_ref[h, :, pl.ds(off, KEY_GROUP)])]
            l = l_ref[h]
            for pc in ps:
                l = l + pc
            l_ref[h] = l
            p = jnp.concatenate(ps, axis=1).astype(BF16)
            acc_ref[h] += jnp.dot(p, v_g, preferred_element_type=F32)
        return carry

    lax.fori_loop(0, ng, pv_group, 0)

    col_q = lax.broadcasted_iota(I32, (TQ, TQ), 1)
    for j in range(N_HEADS_A // 2):
        outs = []
        for h in (2 * j, 2 * j + 1):
            l = jnp.sum(l_ref[h], axis=1, keepdims=True)
            outs.append(acc_ref[h] / l)
        o_ref[0, :, j * LANES:(j + 1) * LANES] = jnp.where(col_q < HEAD_DIM, outs[0], outs[1])


def _attend_prompt(qs, qis, kiw, kib2, kb, vb, topk):
    B, S, _ = qs.shape
    TQ = LANES
    nqb = S // TQ
    idx_bits = int(np.ceil(np.log2(S))) + 1
    qblk = lambda b, i: (b, i, 0)
    seq = lambda b, i: (b, 0, 0)
    return pl.pallas_call(
        functools.partial(_attn_prompt_kernel, topk=topk, idx_bits=idx_bits),
        grid=(B, nqb),
        in_specs=[pl.BlockSpec((1, TQ, D_ATTN), qblk),
                  pl.BlockSpec((1, TQ, D_ATTN), qblk),
                  pl.BlockSpec((1, TQ, LANES), qblk),
                  pl.BlockSpec((1, S, LANES), seq),
                  pl.BlockSpec((1, S, D_ATTN), seq),
                  pl.BlockSpec((1, S, D_ATTN), seq)],
        out_specs=pl.BlockSpec((1, TQ, D_ATTN), qblk),
        out_shape=jax.ShapeDtypeStruct((B, S, D_ATTN), F32),
        scratch_shapes=[pltpu.VMEM((TQ, S), I32),
                        pltpu.VMEM((S, TQ), I32),
                        pltpu.VMEM((TQ, S), F32),
                        pltpu.VMEM((N_HEADS_A, TQ, S), F32),
                        pltpu.VMEM((N_HEADS_A, TQ, TQ), F32),
                        pltpu.VMEM((N_HEADS_A, TQ, TQ), F32),
                        pltpu.VMEM((N_HEADS_A, TQ, LANES), F32),
                        pltpu.VMEM((SUBLANES, TQ), I32)],
        compiler_params=_cparams(("arbitrary", "arbitrary")),
        name="attend_prompt",
    )(qs, qis, kiw, kib2, kb, vb)


K_CHUNK = 8
KV_SLOTS = 3


def _attn_sample_kernel(pt_ref, qs_ref, qis_ref, kiw_ref, knew_ref, vnew_ref,
                        ckidx_hbm, ck_hbm, cv_hbm, o_ref,
                        kibuf, kbuf, vbuf, key_ref, bias_ref, lg_ref, sem_ki, sem_k, sem_v,
                        *, layer, topk, idx_bits, n_pages, page):
    s = pl.program_id(0)
    T = qs_ref.shape[1]
    R = N_HEADS_A * T
    past = n_pages * page
    n_chunks = n_pages // K_CHUNK
    CW = K_CHUNK * page

    def ki_copy(p):
        return pltpu.make_async_copy(ckidx_hbm.at[layer, pt_ref[s, p]],
                                     kibuf.at[:, pl.ds(pl.multiple_of(p * page, page), page)],
                                     sem_ki)

    def kv_copy(hbm, buf, sem, c, slot, j):
        return pltpu.make_async_copy(hbm.at[layer, pt_ref[s, c * K_CHUNK + j]],
                                     buf.at[slot, :, pl.ds(j * page, page)], sem.at[slot])

    def start_chunk(hbm, buf, sem, c):
        for j in range(K_CHUNK):
            kv_copy(hbm, buf, sem, c, c % KV_SLOTS, j).start()

    def wait_chunk(hbm, buf, sem, c):
        for j in range(K_CHUNK):
            kv_copy(hbm, buf, sem, c, c % KV_SLOTS, j).wait()

    def start_ki(p, carry):
        ki_copy(p).start()
        return carry

    lax.fori_loop(0, n_pages, start_ki, 0)
    for c in range(KV_SLOTS):
        start_chunk(ck_hbm, kbuf, sem_k, c)
        start_chunk(cv_hbm, vbuf, sem_v, c)

    row = lax.broadcasted_iota(I32, (T, page), 0)
    col = lax.broadcasted_iota(I32, (T, page), 1)
    new_ok = col <= row

    qis = qis_ref[0]
    parts = []
    for h in range(IDX_HEADS):
        slab = qis[:, (h // 2) * LANES:(h // 2 + 1) * LANES]
        if h % 2:
            slab = pltpu.roll(slab, HEAD_DIM, 1)
        parts.append(slab[:, :IDX_DIM])
    qi_rows = jnp.concatenate(parts, axis=0).astype(BF16)
    w = kiw_ref[0][:, IDX_DIM:N_KW]
    w_rows = jnp.concatenate([jnp.broadcast_to(w[:, h:h + 1], (T, page))
                              for h in range(IDX_HEADS)], axis=0)

    def scores(s_rows):
        n = s_rows.shape[1]
        r = jnp.maximum(s_rows, 0.0) * jnp.concatenate([w_rows] * (n // page), axis=1)
        return jnp.sum(r.reshape(IDX_HEADS, T, n), axis=0)

    def wait_ki(p, carry):
        ki_copy(p).wait()
        return carry

    lax.fori_loop(0, n_pages, wait_ki, 0)

    for c in range(n_chunks):
        ki_c = kibuf[:, c * CW:(c + 1) * CW].astype(BF16)
        s_rows = jnp.dot(qi_rows, ki_c, preferred_element_type=F32)
        key_ref[:, c * CW:(c + 1) * CW] = _sort_key(scores(s_rows))

    zpad = jnp.zeros((page - T, LANES), F32)
    ki_new = jnp.concatenate([kiw_ref[0], zpad], axis=0)[:, :IDX_DIM].astype(BF16)
    key_new = _sort_key(scores(_dot_nt(qi_rows, ki_new)))
    key_ref[:, past:past + page] = jnp.where(new_ok, key_new, INT_MIN)

    keys = key_ref[...]
    lidx = lax.broadcasted_iota(I32, keys.shape, 1)

    def count(m):
        return jnp.sum(m, axis=1, keepdims=True)

    def value_bit(j, thr):
        cand = thr ^ lax.shift_left(jnp.int32(1), 31 - j)
        cnt = count(jnp.where(keys >= cand, 1.0, 0.0))
        return jnp.where(cnt >= topk, cand, thr)

    thr = lax.fori_loop(0, 32, value_bit, jnp.full((T, 1), INT_MIN, I32))
    n_gt = count(jnp.where(keys > thr, 1.0, 0.0))
    n_ge = count(jnp.where(keys >= thr, 1.0, 0.0))
    need = topk - n_gt

    def index_bit(j, lim):
        cand = lim | lax.shift_left(jnp.int32(1), idx_bits - 1 - j)
        cnt = count(jnp.where(keys == thr, jnp.where(lidx < cand, 1.0, 0.0), 0.0))
        return jnp.where(cnt <= need, cand, lim)

    lim = lax.cond(jnp.max(n_ge) > topk,
                   lambda: lax.fori_loop(0, idx_bits, index_bit, jnp.zeros((T, 1), I32)),
                   lambda: jnp.full((T, 1), 2 ** idx_bits, I32))
    visible = (lidx < past) | (lidx - past <= lax.broadcasted_iota(I32, keys.shape, 0))
    tie_ok = (keys == thr) & (lidx < lim)
    sel = visible & ((keys > thr) | tie_ok)
    bias_ref[...] = jnp.where(sel, 0.0, NEG_INF)

    qs = qs_ref[0]
    q_rows = jnp.concatenate([qs] * N_HEADS_A, axis=0)
    rr = lax.broadcasted_iota(I32, q_rows.shape, 0)
    cc = lax.broadcasted_iota(I32, q_rows.shape, 1)
    head_blk = _div_pow2(rr, T) == _div_pow2(cc, HEAD_DIM)
    q_bd = jnp.where(head_blk, q_rows, 0.0).astype(BF16)

    def lane_fold(x, op, init):
        for t in range(x.shape[1] // page):
            init = op(init, x[:, t * page:(t + 1) * page])
        return init

    def put_logits(lg_raw, lo, m):
        n = lg_raw.shape[1]
        b = jnp.concatenate([bias_ref[:, lo:lo + n]] * N_HEADS_A, axis=0)
        lg = lg_raw + b
        lg_ref[:, lo:lo + n] = lg
        return lane_fold(lg, jnp.maximum, m)

    m = jnp.full((R, page), -jnp.inf, F32)
    for c in range(n_chunks):
        wait_chunk(ck_hbm, kbuf, sem_k, c)
        k_c = kbuf[c % KV_SLOTS].astype(BF16)
        m = put_logits(jnp.dot(q_bd, k_c, preferred_element_type=F32), c * CW, m)
        if c + KV_SLOTS < n_chunks:
            start_chunk(ck_hbm, kbuf, sem_k, c + KV_SLOTS)

    zrows = jnp.zeros((page - T, D_ATTN), F32)
    k_new = jnp.concatenate([knew_ref[0], zrows], axis=0).astype(BF16)
    m = put_logits(_dot_nt(q_bd, k_new), past, m)
    m_row = jnp.max(m, axis=1, keepdims=True)

    def probs(lo, n):
        return jnp.exp(lg_ref[:, lo:lo + n] - m_row)

    l = jnp.zeros((R, page), F32)
    acc = jnp.zeros((R, D_ATTN), F32)
    for c in range(n_chunks):
        wait_chunk(cv_hbm, vbuf, sem_v, c)
        v_c = vbuf[c % KV_SLOTS].astype(BF16)
        p = probs(c * CW, CW)
        l = lane_fold(p, jnp.add, l)
        acc = acc + _dot_nt(p.astype(BF16), v_c)
        if c + KV_SLOTS < n_chunks:
            start_chunk(cv_hbm, vbuf, sem_v, c + KV_SLOTS)

    v_new = jnp.concatenate([vnew_ref[0], zrows], axis=0).astype(BF16)
    p = probs(past, page)
    l = l + p
    acc = acc + jnp.dot(p.astype(BF16), v_new, preferred_element_type=F32)
    y = jnp.where(head_blk, acc / jnp.sum(l, axis=1, keepdims=True), 0.0)
    o_ref[0] = jnp.sum(y.reshape(N_HEADS_A, T, D_ATTN), axis=0)


def _attend_sample(page_table, qs, qis, kiw, k_new, v_new, cache_kidx, cache_k, cache_v, layer):
    Bd, T, _ = qs.shape
    n_pages = page_table.shape[1]
    page = cache_k.shape[3]
    past = n_pages * page
    n_keys = past + page
    topk = min(TOPK_MAX, (past + T) // 4)
    idx_bits = int(np.ceil(np.log2(n_keys))) + 1
    R = N_HEADS_A * T
    blk = lambda s, pt: (s, 0, 0)
    grid_spec = pltpu.PrefetchScalarGridSpec(
        num_scalar_prefetch=1,
        grid=(Bd,),
        in_specs=[pl.BlockSpec((1, T, D_ATTN), blk),
                  pl.BlockSpec((1, T, D_ATTN), blk),
                  pl.BlockSpec((1, T, LANES), blk),
                  pl.BlockSpec((1, T, D_ATTN), blk),
                  pl.BlockSpec((1, T, D_ATTN), blk),
                  pl.BlockSpec(memory_space=pl.ANY),
                  pl.BlockSpec(memory_space=pl.ANY),
                  pl.BlockSpec(memory_space=pl.ANY)],
        out_specs=pl.BlockSpec((1, T, D_ATTN), blk),
        scratch_shapes=[pltpu.VMEM((IDX_DIM, past), F32),
                        pltpu.VMEM((KV_SLOTS, D_ATTN, K_CHUNK * page), F32),
                        pltpu.VMEM((KV_SLOTS, D_ATTN, K_CHUNK * page), F32),
                        pltpu.VMEM((T, n_keys), I32),
                        pltpu.VMEM((T, n_keys), F32),
                        pltpu.VMEM((R, n_keys), F32),
                        pltpu.SemaphoreType.DMA(()),
                        pltpu.SemaphoreType.DMA((KV_SLOTS,)),
                        pltpu.SemaphoreType.DMA((KV_SLOTS,))],
    )
    return pl.pallas_call(
        functools.partial(_attn_sample_kernel, layer=layer, topk=topk, idx_bits=idx_bits,
                          n_pages=n_pages, page=page),
        grid_spec=grid_spec,
        out_shape=jax.ShapeDtypeStruct((Bd, T, D_ATTN), F32),
        compiler_params=_cparams(("arbitrary",)),
        name="attend_sample",
    )(page_table, qs, qis, kiw, k_new, v_new, cache_kidx, cache_k, cache_v)


def _window_sums(xp):
    s2 = xp + pltpu.roll(xp, 1, 0)
    s4 = s2 + pltpu.roll(s2, 2, 0)
    s8 = s4 + pltpu.roll(s4, 4, 0)
    s16 = s8 + pltpu.roll(s8, 8, 0)
    return s2, s4, s8, s16


def _pool_select(sums, lane):
    g = _div_pow2(lane, POOL_GROUP_DIM)
    win = jnp.where(g == 0, sums[0], jnp.where(g == 1, sums[1], jnp.where(g == 2, sums[2], sums[3])))
    wlen = jnp.where(g == 0, POOL_WINDOWS[0],
                     jnp.where(g == 1, POOL_WINDOWS[1],
                               jnp.where(g == 2, POOL_WINDOWS[2], POOL_WINDOWS[3])))
    return win, wlen


def _layernorm(x, g):
    xc = x - jnp.mean(x, axis=-1, keepdims=True)
    return (xc * lax.rsqrt(jnp.mean(xc * xc, axis=-1, keepdims=True) + EPS)) * g


def _silu(g):
    return g * (1.0 / (1.0 + jnp.exp(-g)))


def _gate_project(x, ya, yb, yc, rest, wo_ref, np_ref):
    ga = rest[:, 0:D_ATTN]
    gb = rest[:, REST_GB:REST_GB + D_POOL]
    gc = rest[:, REST_GC:REST_GC + D_SGU]
    ycat = jnp.concatenate([ya * _silu(ga), yb * _silu(gb), yc * _silu(gc)], axis=1).astype(BF16)
    o = jnp.dot(ycat, wo_ref[...], preferred_element_type=F32)
    on = (o * lax.rsqrt(jnp.mean(o * o, axis=-1, keepdims=True) + EPS)) * np_ref[...]
    return x + on


def _merge_prompt_kernel(x_ref, ya_ref, rest_ref, halo_ref, pw_ref, ps_ref, sw_ref, sbt_ref,
                         sn_ref, wo_ref, np_ref, y_ref, *, tm, halo_rows):
    i = pl.program_id(1)
    rest = rest_ref[0]
    xb = rest[:, REST_XB:REST_XB + D_POOL]
    u = rest[:, REST_U:REST_U + D_SGU]
    vc = rest[:, REST_VC:REST_VC + D_SGU]

    halo = jnp.where(i > 0, halo_ref[0], 0.0)
    sums = _window_sums(jnp.concatenate([halo, xb], axis=0))
    sums = [s[halo_rows:] for s in sums]
    lane = lax.broadcasted_iota(I32, (tm, D_POOL), 1)
    pos = i * tm + lax.broadcasted_iota(I32, (tm, D_POOL), 0)
    win, wlen = _pool_select(sums, lane)
    cnt = jnp.minimum(wlen, pos + 1).astype(F32)
    pooled = (win / cnt - xb).astype(BF16)
    yb = jnp.dot(pooled, pw_ref[...], preferred_element_type=F32) * ps_ref[...]

    vn = _layernorm(vc, sn_ref[...]).astype(BF16)
    rr = lax.broadcasted_iota(I32, (CHUNK, CHUNK), 0)
    cc = lax.broadcasted_iota(I32, (CHUNK, CHUNK), 1)
    w_tril = [jnp.where(cc <= rr, sw_ref[h], 0.0).astype(BF16) for h in range(SGU_HEADS)]
    b_col = [jnp.broadcast_to(sbt_ref[:, h:h + 1], (CHUNK, LANES)) for h in range(SGU_HEADS)]
    lane_c = lax.broadcasted_iota(I32, (CHUNK, LANES), 1)
    z_chunks = []
    for c in range(tm // CHUNK):
        pairs = []
        for j in range(SGU_HEADS // 2):
            vpair = vn[c * CHUNK:(c + 1) * CHUNK, j * LANES:(j + 1) * LANES]
            z0 = jnp.dot(w_tril[2 * j], vpair, preferred_element_type=F32) + b_col[2 * j]
            z1 = jnp.dot(w_tril[2 * j + 1], vpair, preferred_element_type=F32) + b_col[2 * j + 1]
            pairs.append(jnp.where(lane_c < 64, z0, z1))
        z_chunks.append(jnp.concatenate(pairs, axis=1))
    yc = u * jnp.concatenate(z_chunks, axis=0)

    y_ref[0] = _gate_project(x_ref[0], ya_ref[0], yb, yc, rest, wo_ref, np_ref)


def _merge_prompt(x, ya, rest, pw_bd, ps, sw, sbt, sn, wo, npost, tm):
    B, S, D = x.shape
    halo_rows = 16
    row = lambda b, i: (b, i, 0)
    halo = lambda b, i: (b, jnp.maximum(i * (tm // halo_rows) - 1, 0), REST_XB // D_POOL)
    c2 = lambda b, i: (0, 0)
    c3 = lambda b, i: (0, 0, 0)
    return pl.pallas_call(
        functools.partial(_merge_prompt_kernel, tm=tm, halo_rows=halo_rows),
        grid=(B, S // tm),
        in_specs=[pl.BlockSpec((1, tm, D), row),
                  pl.BlockSpec((1, tm, D_ATTN), row),
                  pl.BlockSpec((1, tm, N_REST), row),
                  pl.BlockSpec((1, halo_rows, D_POOL), halo),
                  pl.BlockSpec((D_POOL, D_POOL), c2),
                  pl.BlockSpec((1, D_POOL), c2),
                  pl.BlockSpec((SGU_HEADS, CHUNK, CHUNK), c3),
                  pl.BlockSpec((CHUNK, SGU_HEADS), c2),
                  pl.BlockSpec((1, D_SGU), c2),
                  pl.BlockSpec((D, D), c2),
                  pl.BlockSpec((1, D), c2)],
        out_specs=pl.BlockSpec((1, tm, D), row),
        out_shape=jax.ShapeDtypeStruct((B, S, D), F32),
        compiler_params=_cparams(("arbitrary", "arbitrary")),
        name="merge_prompt",
    )(x, ya, rest, rest, pw_bd, ps, sw, sbt, sn, wo, npost)


def _merge_sample_kernel(x_ref, ya_ref, rest_ref, st_ref, pw_ref, ps_ref, swt_ref, sbt_ref,
                         sn_ref, wo_ref, np_ref, y_ref, vn_ref, *, n_seq, t_new, pos0):
    rows = n_seq * t_new
    pre = st_ref.shape[1]
    rest = rest_ref[...]
    xb = rest[:, REST_XB:REST_XB + D_POOL]
    u = rest[:, REST_U:REST_U + D_SGU]
    vc = rest[:, REST_VC:REST_VC + D_SGU]

    xp = jnp.concatenate([st_ref[...], xb.reshape(n_seq, t_new, D_POOL)], axis=1)
    sums = _window_sums(xp.reshape(n_seq * (pre + t_new), D_POOL))
    sums = [s.reshape(n_seq, pre + t_new, D_POOL)[:, pre:].reshape(rows, D_POOL) for s in sums]
    lane = lax.broadcasted_iota(I32, (rows, D_POOL), 1)
    r = lax.broadcasted_iota(I32, (rows, D_POOL), 0)
    win, wlen = _pool_select(sums, lane)
    cnt = jnp.minimum(wlen, pos0 + _mod_pow2(r, t_new) + 1).astype(F32)
    pooled = (win / cnt - xb).astype(BF16)
    yb = jnp.dot(pooled, pw_ref[...], preferred_element_type=F32) * ps_ref[...]

    vn = _layernorm(vc, sn_ref[...])
    vn_ref[...] = vn
    vnb = vn.astype(BF16)
    rr = lax.broadcasted_iota(I32, (rows, rows), 0)
    cc = lax.broadcasted_iota(I32, (rows, rows), 1)
    keep = ((_div_pow2(rr, t_new) == _div_pow2(cc, t_new))
            & (_mod_pow2(cc, t_new) <= _mod_pow2(rr, t_new)))
    lane_c = lax.broadcasted_iota(I32, (rows, LANES), 1)
    pairs = []
    for j in range(SGU_HEADS // 2):
        vpair = vnb[:, j * LANES:(j + 1) * LANES]
        zs = []
        for h in (2 * j, 2 * j + 1):
            wt = jnp.where(keep, swt_ref[h], 0.0).astype(BF16)
            zs.append(jnp.dot(wt, vpair, preferred_element_type=F32)
                      + jnp.broadcast_to(sbt_ref[:, h:h + 1], (rows, LANES)))
        pairs.append(jnp.where(lane_c < 64, zs[0], zs[1]))
    yc = u * jnp.concatenate(pairs, axis=1)

    y_ref[...] = _gate_project(x_ref[...], ya_ref[...], yb, yc, rest, wo_ref, np_ref)


def _merge_sample(x, ya, rest, state16, pw_bd, ps, sw_tiled, sbt_tiled, sn, wo, npost,
                  n_seq, t_new, pos0):
    rows, D = x.shape
    full2 = lambda shape: pl.BlockSpec(shape, lambda i: (0, 0))
    full3 = lambda shape: pl.BlockSpec(shape, lambda i: (0, 0, 0))
    return pl.pallas_call(
        functools.partial(_merge_sample_kernel, n_seq=n_seq, t_new=t_new, pos0=pos0),
        grid=(1,),
        in_specs=[full2((rows, D)), full2((rows, D_ATTN)), full2((rows, N_REST)),
                  full3(state16.shape), full2((D_POOL, D_POOL)), full2((1, D_POOL)),
                  full3(sw_tiled.shape), full2(sbt_tiled.shape), full2((1, D_SGU)),
                  full2((D, D)), full2((1, D))],
        out_specs=(full2((rows, D)), full2((rows, D_SGU))),
        out_shape=(jax.ShapeDtypeStruct((rows, D), F32),
                   jax.ShapeDtypeStruct((rows, D_SGU), F32)),
        compiler_params=_cparams(("arbitrary",)),
        name="merge_sample",
    )(x, ya, rest, state16, pw_bd, ps, sw_tiled, sbt_tiled, sn, wo, npost)


def _rope_tables(pos):
    rot = HEAD_DIM // 4
    half = rot // 2
    n = pos.shape[0]
    inv = ROPE_THETA ** (-jnp.arange(half, dtype=F32) * 2.0 / rot)
    ang = pos.astype(F32)[:, None] * inv[None, :]
    cos, sin = jnp.cos(ang), jnp.sin(ang)
    one = jnp.ones((n, HEAD_DIM - rot), F32)
    zero = jnp.zeros((n, HEAD_DIM - rot), F32)
    z8 = jnp.zeros((n, half), F32)
    c = jnp.concatenate([cos, cos, one], axis=1)
    a = jnp.concatenate([-sin, z8, zero], axis=1)
    b = jnp.concatenate([z8, sin, zero], axis=1)
    return tuple(jnp.concatenate([t, t], axis=1) for t in (c, a, b))


def _block_diag(w):
    G, n, _ = w.shape
    eye = jnp.eye(G, dtype=w.dtype)
    return (eye[:, None, :, None] * w[:, :, None, :]).reshape(G * n, G * n)


def kernel(x_prompt, x_sample, cache_k, cache_v, cache_kidx, state_pool, page_table, norm_pre,
           w_in, pool_w, pool_scale, sgu_w, sgu_b, sgu_norm, w_out, norm_post):
    B, S, D = x_prompt.shape
    Bd, Ts, _ = x_sample.shape
    depth = w_in.shape[0]
    n_pool, page = cache_k.shape[1], cache_k.shape[2]
    past = page_table.shape[1] * page
    rows_s = Bd * Ts

    tabs_p = _rope_tables(jnp.arange(S, dtype=I32))
    tabs_s = tuple(jnp.tile(t, (Bd, 1)) for t in _rope_tables(past + jnp.arange(Ts, dtype=I32)))

    n_kw_end = N_FRONT + N_KW
    w_packed = jnp.concatenate(
        [w_in[:, :, :n_kw_end], jnp.zeros((depth, D, LANES - N_KW), w_in.dtype), w_in[:, :, n_kw_end:]],
        axis=2).astype(BF16)
    w_out_b = w_out.astype(BF16)
    ck = jnp.transpose(cache_k, (0, 1, 3, 4, 2)).reshape(depth, n_pool, D_ATTN, page)
    cv = jnp.transpose(cache_v, (0, 1, 3, 4, 2)).reshape(depth, n_pool, D_ATTN, page)
    cki = jnp.transpose(cache_kidx, (0, 1, 3, 2))

    xp = x_prompt
    xs = x_sample.reshape(1, rows_s, D)
    outs = {n: [] for n in ("kp", "vp", "kip", "pp", "ks", "vs", "kis", "ps", "cvs")}
    topk_p = min(TOPK_MAX, S // 4)
    for l in range(depth):
        g_pre = norm_pre[l].reshape(1, D)
        g_post = norm_post[l].reshape(1, D)
        pw_bd = _block_diag(pool_w[l]).astype(BF16)
        ps = pool_scale[l].reshape(1, D_POOL)
        sn = sgu_norm[l].reshape(1, D_SGU)

        qs, k, kb, v, vb, qis, kiw, kib2, rest = _project(xp, g_pre, w_packed[l], tabs_p, 256, BF16)
        ya = _attend_prompt(qs, qis, kiw, kib2, kb, vb, topk_p)
        xp = _merge_prompt(xp, ya, rest, pw_bd, ps, sgu_w[l], sgu_b[l].T, sn, w_out_b[l], g_post, 256)
        outs["kp"].append(k.reshape(B, S, N_HEADS_A, HEAD_DIM))
        outs["vp"].append(v.reshape(B, S, N_HEADS_A, HEAD_DIM))
        outs["kip"].append(kiw[:, :, :IDX_DIM])
        outs["pp"].append(rest[:, S - POOL_STATE:, REST_XB:REST_XB + D_POOL])

        qs, k, kb, v, vb, qis, kiw, kib2, rest = _project(xs, g_pre, w_packed[l], tabs_s, rows_s, F32)
        seq = lambda t: t.reshape(Bd, Ts, t.shape[-1])
        ya = _attend_sample(page_table, seq(qs), seq(qis), seq(kiw), seq(k), seq(v),
                            cki, ck, cv, l)
        state16 = jnp.pad(state_pool[l], ((0, 0), (1, 0), (0, 0)))
        sw_tiled = jnp.tile(sgu_w[l][:, :Ts, :Ts], (1, Bd, Bd))
        sbt_tiled = jnp.tile(sgu_b[l][:, :Ts].T, (Bd, 1))
        xs2, vn = _merge_sample(xs[0], ya.reshape(rows_s, D_ATTN), rest[0], state16, pw_bd, ps,
                                sw_tiled, sbt_tiled, sn, w_out_b[l], g_post, Bd, Ts, past)
        xs = xs2.reshape(1, rows_s, D)
        xb_s = rest[0][:, REST_XB:REST_XB + D_POOL].reshape(Bd, Ts, D_POOL)
        outs["ks"].append(k.reshape(Bd, Ts, N_HEADS_A, HEAD_DIM))
        outs["vs"].append(v.reshape(Bd, Ts, N_HEADS_A, HEAD_DIM))
        outs["kis"].append(kiw[0][:, :IDX_DIM].reshape(Bd, Ts, IDX_DIM))
        outs["ps"].append(jnp.concatenate([state_pool[l], xb_s], axis=1)[:, -POOL_STATE:])
        outs["cvs"].append(vn.reshape(Bd, Ts, D_SGU))

    st = lambda n: jnp.stack(outs[n])
    return (xp, xs.reshape(Bd, Ts, D), st("kp"), st("vp"), st("kip"), st("pp"),
            st("ks"), st("vs"), st("kis"), st("ps"), st("cvs"))
```
